```python
import math
import jax, jax.numpy as jnp
from jax import lax
import numpy as np

D_MODEL = 1024
BATCH = 8
SEQ = 2048
DEPTH = 4
DEC_BATCH = 128
DEC_SEQ = 1
PAST_LEN = 8192
PAGE_SIZE = 128

N_LAYERS_A = (DEPTH + 1) // 2
N_LAYERS_B = DEPTH // 2
N_HEADS_A = 8
N_KV_HEADS_A = 4
GROUP_A = N_HEADS_A // N_KV_HEADS_A
HEAD_DIM_A = D_MODEL // (2 * N_HEADS_A)
V_DIM_A = 2 * HEAD_DIM_A
KV_ROW_A = 2 * HEAD_DIM_A + V_DIM_A
N_HEADS_B = 16
QK_NOPE_B = 64
QK_ROPE_B = 32
V_DIM_B = 64
Q_LORA_B = 256
KV_LORA_B = 128
KV_ROW_B = KV_LORA_B + QK_ROPE_B
ROPE_THETA = 10000.0
MLA_SCALE = (QK_NOPE_B + QK_ROPE_B) ** -0.5
D_FF = 2816
CONV_W = 3
NUM_BUCKETS = 32
MAX_DISTANCE = 128
Q_BLOCK = 128
RMS_EPS = 1e-6
NEG_INF = -1e30

kernel_name = "diffattn_mla_convglu_hybrid_step"


def _rmsnorm(x, g):
    xf = x.astype(jnp.float32)
    y = xf * lax.rsqrt(jnp.mean(xf * xf, -1, keepdims=True) + RMS_EPS)
    return (y * g.astype(jnp.float32)).astype(x.dtype)


def _t5_bias(q_pos, k_pos, table):
    n = jnp.maximum(q_pos[:, None] - k_pos[None, :], 0)
    max_exact = NUM_BUCKETS // 2
    nf = jnp.maximum(n, 1).astype(jnp.float32)
    large = max_exact + (jnp.log(nf / max_exact) / math.log(MAX_DISTANCE / max_exact)
                         * (NUM_BUCKETS - max_exact)).astype(jnp.int32)
    bucket = jnp.where(n < max_exact, n, jnp.minimum(large, NUM_BUCKETS - 1))
    b = table.astype(jnp.float32)[bucket]
    return jnp.moveaxis(b, -1, 0).reshape(N_KV_HEADS_A, GROUP_A, *bucket.shape)


def _rope(x, pos):
    half = QK_ROPE_B // 2
    inv = ROPE_THETA ** (-jnp.arange(half, dtype=jnp.float32) * 2.0 / QK_ROPE_B)
    ang = pos.astype(jnp.float32)[:, None] * inv[None, :]
    ang = ang.reshape(ang.shape[0], *([1] * (x.ndim - 3)), half)
    cos, sin = jnp.cos(ang), jnp.sin(ang)
    xf = x.astype(jnp.float32)
    x1, x2 = xf[..., :half], xf[..., half:]
    return jnp.concatenate([x1 * cos - x2 * sin, x1 * sin + x2 * cos], -1).astype(x.dtype)


def _to_blocks(q):
    n, t = q.shape[:2]
    return jnp.moveaxis(q.reshape(n, t // Q_BLOCK, Q_BLOCK, *q.shape[2:]), 1, 0)


def _from_blocks(o):
    nb, n, qb = o.shape[:3]
    return jnp.moveaxis(o, 0, 1).reshape(n, nb * qb, *o.shape[3:])


def _partial(s, v, eq):
    m = jnp.max(s, -1)
    p = jnp.exp(s - m[..., None])
    return m, jnp.sum(p, -1), jnp.einsum(eq, p, v.astype(jnp.float32))


def _merge(m, l, acc):
    mx = jnp.max(m, 0)
    w = jnp.exp(m - mx)
    return jnp.sum(w[..., None] * acc, 0) / jnp.sum(w * l, 0)[..., None]


def _diff_project(h, w_q, w_kv):
    n, t, _ = h.shape
    q = (h @ w_q).reshape(n, t, N_KV_HEADS_A, GROUP_A, 2, HEAD_DIM_A)
    rows = (h @ w_kv).reshape(n, t, N_KV_HEADS_A, KV_ROW_A)
    return q[..., 0, :], q[..., 1, :], rows


def _diff_scores(q1, q2, rows, bias):
    k1 = rows[..., :HEAD_DIM_A]
    k2 = rows[..., HEAD_DIM_A:2 * HEAD_DIM_A]
    v = rows[..., 2 * HEAD_DIM_A:]
    scale = HEAD_DIM_A ** -0.5
    s1 = jnp.einsum('ntkgd,nskd->nkgts', q1, k1).astype(jnp.float32) * scale + bias
    s2 = jnp.einsum('ntkgd,nskd->nkgts', q2, k2).astype(jnp.float32) * scale + bias
    return s1, s2, v


def _diff_lambda(lq1, lk1, lq2, lk2, lam_init):
    f = lambda a: a.astype(jnp.float32)
    return jnp.exp(jnp.sum(f(lq1) * f(lk1))) - jnp.exp(jnp.sum(f(lq2) * f(lk2))) + lam_init


def _diff_out(o, subln, w_o, lam_init, dtype):
    n, t = o.shape[:2]
    o = _rmsnorm(o, subln) * (1.0 - lam_init)
    return o.reshape(n, t, N_HEADS_A * V_DIM_A).astype(dtype) @ w_o


def diff_attn_prompt(h, w_q, w_kv, lq1, lk1, lq2, lk2, subln, w_o, rel_bias, lam_init):
    n, t, _ = h.shape
    q1, q2, rows = _diff_project(h, w_q, w_kv)
    lam = _diff_lambda(lq1, lk1, lq2, lk2, lam_init)
    k_pos = jnp.arange(t)
    v32 = rows[..., 2 * HEAD_DIM_A:].astype(jnp.float32)

    def block(args):
        i, qa, qb = args
        q_pos = i * Q_BLOCK + jnp.arange(Q_BLOCK)
        mask = k_pos[None, :] <= q_pos[:, None]
        s1, s2, _ = _diff_scores(qa, qb, rows, _t5_bias(q_pos, k_pos, rel_bias))
        a = (jax.nn.softmax(jnp.where(mask, s1, NEG_INF), -1)
             - lam * jax.nn.softmax(jnp.where(mask, s2, NEG_INF), -1))
        return jnp.einsum('nkgts,nskd->ntkgd', a, v32)

    o = lax.map(block, (jnp.arange(t // Q_BLOCK), _to_blocks(q1), _to_blocks(q2)))
    return _diff_out(_from_blocks(o), subln, w_o, lam_init, h.dtype), rows


def diff_attn_sample(h, cache, layer, page_table, w_q, w_kv, lq1, lk1, lq2, lk2, subln, w_o,
                     rel_bias, lam_init):
    n, t, _ = h.shape
    q1, q2, rows = _diff_project(h, w_q, w_kv)
    lam = _diff_lambda(lq1, lk1, lq2, lk2, lam_init)
    q_pos = PAST_LEN + jnp.arange(t)
    eq = 'nkgts,nskd->nkgtd'

    def page(args):
        j, ids = args
        past = cache[layer, ids]
        bias = _t5_bias(q_pos, j * PAGE_SIZE + jnp.arange(PAGE_SIZE), rel_bias)
        s1, s2, v = _diff_scores(q1, q2, past, bias)
        return _partial(s1, v, eq) + _partial(s2, v, eq)

    parts = lax.map(page, (jnp.arange(page_table.shape[1]), page_table.T))
    mask = q_pos[None, :] <= q_pos[:, None]
    s1, s2, v = _diff_scores(q1, q2, rows, _t5_bias(q_pos, q_pos, rel_bias))
    own = (_partial(jnp.where(mask, s1, NEG_INF), v, eq)
           + _partial(jnp.where(mask, s2, NEG_INF), v, eq))
    cat = [jnp.concatenate([p, o[None]], 0) for p, o in zip(parts, own)]
    o = _merge(cat[0], cat[1], cat[2]) - lam * _merge(cat[3], cat[4], cat[5])
    return _diff_out(jnp.moveaxis(o, 3, 1), subln, w_o, lam_init, h.dtype), rows


def _mla_project(h, pos, w_dq, g_q, w_uq, w_dkv, g_kv, w_uk):
    n, t, _ = h.shape
    cq = _rmsnorm(h @ w_dq, g_q)
    q = (cq @ w_uq).reshape(n, t, N_HEADS_B, QK_NOPE_B + QK_ROPE_B)
    q_rope = _rope(q[..., QK_NOPE_B:], pos)
    q_lat = jnp.einsum('nthd,chd->nthc', q[..., :QK_NOPE_B], w_uk)
    q_cat = jnp.concatenate([q_lat, q_rope], -1)
    kv = h @ w_dkv
    rows = jnp.concatenate([_rmsnorm(kv[..., :KV_LORA_B], g_kv),
                            _rope(kv[..., KV_LORA_B:], pos)], -1)
    return q_cat, rows


def _mla_scores(q_cat, rows):
    return jnp.einsum('nthc,nsc->nhts', q_cat, rows).astype(jnp.float32) * MLA_SCALE


def _mla_out(o_lat, w_uv, w_o, dtype):
    n, t = o_lat.shape[:2]
    o = jnp.einsum('nthc,chd->nthd', o_lat.astype(dtype), w_uv)
    return o.reshape(n, t, N_HEADS_B * V_DIM_B) @ w_o


def mla_prompt(h, pos, w_dq, g_q, w_uq, w_dkv, g_kv, w_uk, w_uv, w_o):
    n, t, _ = h.shape
    q_cat, rows = _mla_project(h, pos, w_dq, g_q, w_uq, w_dkv, g_kv, w_uk)
    c_kv = rows[..., :KV_LORA_B].astype(jnp.float32)
    k_pos = jnp.arange(t)

    def block(args):
        i, qb = args
        q_pos = i * Q_BLOCK + jnp.arange(Q_BLOCK)
        mask = k_pos[None, :] <= q_pos[:, None]
        p = jax.nn.softmax(jnp.where(mask, _mla_scores(qb, rows), NEG_INF), -1)
        return jnp.einsum('nhts,nsc->nthc', p, c_kv)

    o = lax.map(block, (jnp.arange(t // Q_BLOCK), _to_blocks(q_cat)))
    return _mla_out(_from_blocks(o), w_uv, w_o, h.dtype), rows


def mla_sample(h, pos, cache, layer, page_table, w_dq, g_q, w_uq, w_dkv, g_kv, w_uk, w_uv, w_o):
    q_cat, rows = _mla_project(h, pos, w_dq, g_q, w_uq, w_dkv, g_kv, w_uk)
    eq = 'nhts,nsc->nhtc'

    def page(ids):
        past = cache[layer, ids]
        return _partial(_mla_scores(q_cat, past), past[..., :KV_LORA_B], eq)

    m, l, acc = lax.map(page, page_table.T)
    mask = pos[None, :] <= pos[:, None]
    om, ol, oa = _partial(jnp.where(mask, _mla_scores(q_cat, rows), NEG_INF),
                          rows[..., :KV_LORA_B], eq)
    o = _merge(jnp.concatenate([m, om[None]], 0), jnp.concatenate([l, ol[None]], 0),
               jnp.concatenate([acc, oa[None]], 0))
    return _mla_out(jnp.moveaxis(o, 2, 1), w_uv, w_o, h.dtype), rows


def conv_ffn(h, prev, w_gate, w_up, w_conv, b_conv, w_down):
    t = h.shape[1]
    g = h @ w_gate
    u = h @ w_up
    gx = jnp.concatenate([prev.astype(g.dtype), g], 1)
    c = b_conv
    for k in range(CONV_W):
        c = c + w_conv[k] * gx[:, k:k + t]
    y = (jax.nn.silu(c) * u) @ w_down
    return y, gx[:, t:]


def setup_inputs(seed: int = 0) -> dict:
    key = jax.random.key(seed)
    ks = iter(jax.random.split(key, 40))

    def nrm(shape, scale=None):
        z = jax.random.normal(next(ks), shape, jnp.float32)
        return z if scale is None else z * scale

    n_pages = PAST_LEN // PAGE_SIZE
    n_used = DEC_BATCH * n_pages
    n_phys = n_used + max(1, n_used // 4)
    inp = {}
    inp["x_prompt"] = nrm((BATCH, SEQ, D_MODEL))
    inp["x_sample"] = nrm((DEC_BATCH, DEC_SEQ, D_MODEL))
    inp["cache_diff_kv"] = nrm((N_LAYERS_A, n_phys, PAGE_SIZE, N_KV_HEADS_A, KV_ROW_A))
    inp["cache_mla"] = nrm((N_LAYERS_B, n_phys, PAGE_SIZE, KV_ROW_B))
    inp["state_ffn_conv"] = nrm((DEPTH, DEC_BATCH, CONV_W - 1, D_FF))
    inp["page_table"] = jax.random.permutation(next(ks), n_phys)[:n_used].reshape(
        DEC_BATCH, n_pages).astype(jnp.int32)
    inp["rel_bias"] = nrm((NUM_BUCKETS, N_HEADS_A), 0.5)
    inp["g_attn"] = 1.0 + nrm((DEPTH, D_MODEL), 0.02)
    inp["g_ffn"] = 1.0 + nrm((DEPTH, D_MODEL), 0.02)
    inp["g_final"] = 1.0 + nrm((D_MODEL,), 0.02)
    inp["w_q_a"] = nrm((N_LAYERS_A, D_MODEL, N_HEADS_A * 2 * HEAD_DIM_A), D_MODEL ** -0.5)
    inp["w_kv_a"] = nrm((N_LAYERS_A, D_MODEL, N_KV_HEADS_A * KV_ROW_A), D_MODEL ** -0.5)
    inp["lam_q1"] = nrm((N_LAYERS_A, HEAD_DIM_A), 0.1)
    inp["lam_k1"] = nrm((N_LAYERS_A, HEAD_DIM_A), 0.1)
    inp["lam_q2"] = nrm((N_LAYERS_A, HEAD_DIM_A), 0.1)
    inp["lam_k2"] = nrm((N_LAYERS_A, HEAD_DIM_A), 0.1)
    inp["subln_a"] = 1.0 + nrm((N_LAYERS_A, V_DIM_A), 0.02)
    inp["w_o_a"] = nrm((N_LAYERS_A, N_HEADS_A * V_DIM_A, D_MODEL), (N_HEADS_A * V_DIM_A) ** -0.5)
    inp["w_dq_b"] = nrm((N_LAYERS_B, D_MODEL, Q_LORA_B), D_MODEL ** -0.5)
    inp["g_q_b"] = 1.0 + nrm((N_LAYERS_B, Q_LORA_B), 0.02)
    inp["w_uq_b"] = nrm((N_LAYERS_B, Q_LORA_B, N_HEADS_B * (QK_NOPE_B + QK_ROPE_B)), Q_LORA_B ** -0.5)
    inp["w_dkv_b"] = nrm((N_LAYERS_B, D_MODEL, KV_ROW_B), D_MODEL ** -0.5)
    inp["g_kv_b"] = 1.0 + nrm((N_LAYERS_B, KV_LORA_B), 0.02)
    inp["w_uk_b"] = nrm((N_LAYERS_B, KV_LORA_B, N_HEADS_B, QK_NOPE_B), KV_LORA_B ** -0.5)
    inp["w_uv_b"] = nrm((N_LAYERS_B, KV_LORA_B, N_HEADS_B, V_DIM_B), KV_LORA_B ** -0.5)
    inp["w_o_b"] = nrm((N_LAYERS_B, N_HEADS_B * V_DIM_B, D_MODEL), (N_HEADS_B * V_DIM_B) ** -0.5)
    inp["w_gate"] = nrm((DEPTH, D_MODEL, D_FF), D_MODEL ** -0.5)
    inp["w_up"] = nrm((DEPTH, D_MODEL, D_FF), D_MODEL ** -0.5)
    inp["w_conv"] = nrm((DEPTH, CONV_W, D_FF), CONV_W ** -0.5)
    inp["b_conv"] = nrm((DEPTH, D_FF), 0.01)
    inp["w_down"] = nrm((DEPTH, D_FF, D_MODEL), D_FF ** -0.5)
    return inp


def reference(x_prompt, x_sample, cache_diff_kv, cache_mla, state_ffn_conv, page_table,
              rel_bias, g_attn, g_ffn, g_final,
              w_q_a, w_kv_a, lam_q1, lam_k1, lam_q2, lam_k2, subln_a, w_o_a,
              w_dq_b, g_q_b, w_uq_b, w_dkv_b, g_kv_b, w_uk_b, w_uv_b, w_o_b,
              w_gate, w_up, w_conv, b_conv, w_down):
    xp, xs = x_prompt, x_sample
    pos_p = jnp.arange(xp.shape[1])
    pos_s = PAST_LEN + jnp.arange(xs.shape[1])
    diff_p, diff_s, mla_p, mla_s, conv_p, conv_s = [], [], [], [], [], []
    for i in range(DEPTH):
        hp = _rmsnorm(xp, g_attn[i])
        hs = _rmsnorm(xs, g_attn[i])
        j = i // 2
        if i % 2 == 0:
            lam_init = 0.8 - 0.6 * math.exp(-0.3 * i)
            wa = (w_q_a[j], w_kv_a[j], lam_q1[j], lam_k1[j], lam_q2[j], lam_k2[j], subln_a[j], w_o_a[j])
            yp, rp = diff_attn_prompt(hp, *wa, rel_bias, lam_init)
            ys, rs = diff_attn_sample(hs, cache_diff_kv, j, page_table, *wa, rel_bias, lam_init)
            diff_p.append(rp)
            diff_s.append(rs)
        else:
            wb = (w_dq_b[j], g_q_b[j], w_uq_b[j], w_dkv_b[j], g_kv_b[j], w_uk_b[j], w_uv_b[j], w_o_b[j])
            yp, rp = mla_prompt(hp, pos_p, *wb)
            ys, rs = mla_sample(hs, pos_s, cache_mla, j, page_table, *wb)
            mla_p.append(rp)
            mla_s.append(rs)
        xp = xp + yp
        xs = xs + ys
        wf = (w_gate[i], w_up[i], w_conv[i], b_conv[i], w_down[i])
        prev_p = jnp.zeros((xp.shape[0], CONV_W - 1, D_FF), xp.dtype)
        yp, cp = conv_ffn(_rmsnorm(xp, g_ffn[i]), prev_p, *wf)
        ys, cs = conv_ffn(_rmsnorm(xs, g_ffn[i]), state_ffn_conv[i], *wf)
        conv_p.append(cp)
        conv_s.append(cs)
        xp = xp + yp
        xs = xs + ys
    y_prompt = _rmsnorm(xp, g_final)
    y_sample = _rmsnorm(xs, g_final)
    new_diff_kv_prompt = jnp.stack(diff_p)
    new_diff_kv_sample = jnp.stack(diff_s)
    new_mla_prompt = jnp.stack(mla_p)
    new_mla_sample = jnp.stack(mla_s)
    new_conv_prompt = jnp.stack(conv_p)
    new_conv_sample = jnp.stack(conv_s)
    return (y_prompt, y_sample, new_diff_kv_prompt, new_diff_kv_sample,
            new_mla_prompt, new_mla_sample, new_conv_prompt, new_conv_sample)
```

```python
import functools
import math

import jax
import jax.numpy as jnp
from jax import lax
from jax.experimental import pallas as pl
from jax.experimental.pallas import tpu as pltpu

F32 = jnp.float32
BF16 = jnp.bfloat16

N_KV_HEADS_A = 4
GROUP_A = 2
N_HEADS_A = N_KV_HEADS_A * GROUP_A
HEAD_DIM_A = 64
V_DIM_A = 2 * HEAD_DIM_A
KV_ROW_A = 2 * HEAD_DIM_A + V_DIM_A
N_HEADS_B = 16
QK_NOPE_B = 64
QK_ROPE_B = 32
V_DIM_B = 64
KV_LORA_B = 128
KV_ROW_B = KV_LORA_B + QK_ROPE_B
ROPE_THETA = 10000.0
MLA_SCALE = (QK_NOPE_B + QK_ROPE_B) ** -0.5
CONV_W = 3
NUM_BUCKETS = 32
MAX_DISTANCE = 128
RMS_EPS = 1e-6
NEG_INF = -1e30
PAGE_SIZE = 128

LANES = 128
BF16_SUBLANES = 16
VMEM_LIMIT = 48 * 1024 * 1024
ROW_TILE = 512
FFN_ROW_TILE = 1024
FFN_COL_TILE = 256
FLASH_B = 512
MLA_BQ = 128
MLA_BK = 512
DIFF_PAGES_PER_STEP = 8
MLA_PAGES_PER_STEP = 16


def _cparams(sem):
    return pltpu.CompilerParams(dimension_semantics=sem, vmem_limit_bytes=VMEM_LIMIT)


def _rms(x, g):
    return x * lax.rsqrt(jnp.mean(x * x, -1, keepdims=True) + RMS_EPS) * g


def _dot(a, b):
    return jnp.dot(a, b, preferred_element_type=F32)


def _dot_nt(a, b):
    return lax.dot_general(a, b, (((1,), (1,)), ((), ())), preferred_element_type=F32)


def _row_tile(m, want):
    t = min(m, want)
    assert m % t == 0, (m, t)
    return t


def _norm_linear_kernel(x_ref, g_ref, *refs, scales):
    n_w = len(scales)
    h = _rms(x_ref[...], g_ref[...]).astype(BF16)
    for w_ref, o_ref, sc in zip(refs[:n_w], refs[n_w:], scales):
        acc = _dot(h, w_ref[...])
        if sc != 1.0:
            acc = acc * sc
        o_ref[...] = acc.astype(o_ref.dtype)


def norm_linear(x, g, ws, scales, dtypes):
    m, d = x.shape
    bm = _row_tile(m, ROW_TILE)
    in_specs = [pl.BlockSpec((bm, d), lambda i: (i, 0)), pl.BlockSpec((1, d), lambda i: (0, 0))]
    in_specs += [pl.BlockSpec(w.shape, lambda i: (0, 0)) for w in ws]
    return pl.pallas_call(
        functools.partial(_norm_linear_kernel, scales=tuple(scales)),
        grid=(m // bm,),
        in_specs=in_specs,
        out_specs=[pl.BlockSpec((bm, w.shape[1]), lambda i: (i, 0)) for w in ws],
        out_shape=[jax.ShapeDtypeStruct((m, w.shape[1]), dt) for w, dt in zip(ws, dtypes)],
        compiler_params=_cparams(("parallel",)),
        name="norm_linear",
    )(x, g.reshape(1, d), *ws)


def _linear_residual_kernel(a_ref, w_ref, x_ref, o_ref):
    o_ref[...] = x_ref[...] + _dot(a_ref[...], w_ref[...])


def linear_residual(a, w, x):
    m, k = a.shape
    d = w.shape[1]
    bm = _row_tile(m, ROW_TILE)
    return pl.pallas_call(
        _linear_residual_kernel,
        grid=(m // bm,),
        in_specs=[pl.BlockSpec((bm, k), lambda i: (i, 0)), pl.BlockSpec((k, d), lambda i: (0, 0)),
                  pl.BlockSpec((bm, d), lambda i: (i, 0))],
        out_specs=pl.BlockSpec((bm, d), lambda i: (i, 0)),
        out_shape=jax.ShapeDtypeStruct((m, d), F32),
        compiler_params=_cparams(("parallel",)),
        name="linear_residual",
    )(a, w, x)


def _rmsnorm_kernel(x_ref, g_ref, o_ref):
    o_ref[...] = _rms(x_ref[...], g_ref[...])


def rmsnorm(x, g):
    m, d = x.shape
    bm = _row_tile(m, ROW_TILE)
    return pl.pallas_call(
        _rmsnorm_kernel,
        grid=(m // bm,),
        in_specs=[pl.BlockSpec((bm, d), lambda i: (i, 0)), pl.BlockSpec((1, d), lambda i: (0, 0))],
        out_specs=pl.BlockSpec((bm, d), lambda i: (i, 0)),
        out_shape=jax.ShapeDtypeStruct((m, d), F32),
        compiler_params=_cparams(("parallel",)),
        name="final_rmsnorm",
    )(x, g.reshape(1, d))


def _bucket(dist):
    n = jnp.maximum(dist, 0)
    max_exact = NUM_BUCKETS // 2
    nf = jnp.maximum(n, 1).astype(F32)
    large = max_exact + (jnp.log(nf / max_exact) / math.log(MAX_DISTANCE / max_exact)
                         * (NUM_BUCKETS - max_exact)).astype(jnp.int32)
    return jnp.where(n < max_exact, n, jnp.minimum(large, NUM_BUCKETS - 1))


def _lookup(bucket, rel_ref, h):
    val = jnp.zeros(bucket.shape, F32)
    for k in range(NUM_BUCKETS):
        val = jnp.where(bucket == k, rel_ref[k, h], val)
    return val


def _bias_kernel(rel_ref, tile_ref, dec_ref, *, blk, n_pages):
    h = pl.program_id(0)
    row = lax.broadcasted_iota(jnp.int32, (blk, blk), 0)
    col = lax.broadcasted_iota(jnp.int32, (blk, blk), 1)
    tile_ref[0, 0] = _lookup(_bucket(row - col), rel_ref, h)
    tile_ref[0, 1] = _lookup(_bucket(blk + row - col), rel_ref, h)
    page = lax.broadcasted_iota(jnp.int32, (n_pages + 1, PAGE_SIZE), 0)
    slot = lax.broadcasted_iota(jnp.int32, (n_pages + 1, PAGE_SIZE), 1)
    past = n_pages * PAGE_SIZE
    dist = jnp.where(page < n_pages, past - (page * PAGE_SIZE + slot), 0)
    dec_ref[0] = _lookup(_bucket(dist), rel_ref, h)


def bias_tables(rel_bias, blk, n_pages):
    return pl.pallas_call(
        functools.partial(_bias_kernel, blk=blk, n_pages=n_pages),
        grid=(N_HEADS_A,),
        in_specs=[pl.BlockSpec(memory_space=pltpu.SMEM)],
        out_specs=[pl.BlockSpec((1, 2, blk, blk), lambda h: (h, 0, 0, 0)),
                   pl.BlockSpec((1, n_pages + 1, PAGE_SIZE), lambda h: (h, 0, 0))],
        out_shape=[jax.ShapeDtypeStruct((N_HEADS_A, 2, blk, blk), F32),
                   jax.ShapeDtypeStruct((N_HEADS_A, n_pages + 1, PAGE_SIZE), F32)],
        compiler_params=_cparams(("parallel",)),
        name="t5_bias_tables",
    )(rel_bias)


def _diff_lambda(lam_ref, lam_init):
    a = jnp.sum(lam_ref[0:1, :] * lam_ref[1:2, :], axis=1, keepdims=True)
    b = jnp.sum(lam_ref[2:3, :] * lam_ref[3:4, :], axis=1, keepdims=True)
    return jnp.exp(a) - jnp.exp(b) + lam_init


def _flash_diff_kernel(rel_ref, q_ref, kv_ref, bias_ref, lam_ref, sub_ref, o_ref,
                       qs, m_s, l_s, acc_s, *, blk, lam_init):
    c = pl.program_id(1)
    qi = pl.program_id(2)
    kj = pl.program_id(3)
    n_seg = 2 * GROUP_A

    @pl.when(kj == 0)
    def _init():
        lo = lax.broadcasted_iota(jnp.int32, (blk, 2 * HEAD_DIM_A), 1) < HEAD_DIM_A
        zero = jnp.zeros((blk, 2 * HEAD_DIM_A), BF16)
        for g in range(GROUP_A):
            qg = q_ref[0, :, g * 2 * HEAD_DIM_A:(g + 1) * 2 * HEAD_DIM_A]
            qs[(2 * g) * blk:(2 * g + 1) * blk, :] = jnp.where(lo, qg, zero)
            qs[(2 * g + 1) * blk:(2 * g + 2) * blk, :] = jnp.where(lo, zero, qg)
        m_s[...] = jnp.full(m_s.shape, NEG_INF, F32)
        l_s[...] = jnp.zeros(l_s.shape, F32)
        acc_s[...] = jnp.zeros(acc_s.shape, F32)

    def step(mode):
        kk = kv_ref[0, :, :2 * HEAD_DIM_A].astype(BF16)
        v = kv_ref[0, :, 2 * HEAD_DIM_A:].astype(BF16)
        s = _dot_nt(qs[...], kk)
        parts = []
        for g in range(GROUP_A):
            sg = s[2 * g * blk:(2 * g + 2) * blk, :].reshape(2, blk, blk)
            if mode == "far":
                sg = sg + rel_ref[NUM_BUCKETS - 1, c * GROUP_A + g]
            else:
                sg = sg + bias_ref[g, 0][None]
            if mode == "diag":
                row = lax.broadcasted_iota(jnp.int32, (blk, blk), 0)
                col = lax.broadcasted_iota(jnp.int32, (blk, blk), 1)
                sg = jnp.where((col <= row)[None], sg, NEG_INF)
            parts.append(sg.reshape(2 * blk, blk))
        s = jnp.concatenate(parts, axis=0)
        m_prev = m_s[...]
        m_new = jnp.maximum(m_prev, jnp.max(s, axis=1, keepdims=True))
        alpha = jnp.exp(m_prev - m_new)
        p = jnp.exp(s - m_new)
        l_s[...] = alpha * l_s[...] + jnp.sum(p, axis=1, keepdims=True)
        acc_s[...] = alpha * acc_s[...] + _dot(p.astype(BF16), v)
        m_s[...] = m_new

    @pl.when(kj < qi - 1)
    def _far():
        step("far")

    @pl.when(kj == qi - 1)
    def _near():
        step("near")

    @pl.when(kj == qi)
    def _diag():
        step("diag")
        lam = _diff_lambda(lam_ref, lam_init)
        o = acc_s[...] / l_s[...]
        for g in range(GROUP_A):
            o1 = o[(2 * g) * blk:(2 * g + 1) * blk]
            o2 = o[(2 * g + 1) * blk:(2 * g + 2) * blk]
            og = _rms(o1 - lam * o2, sub_ref[...]) * (1.0 - lam_init)
            o_ref[0, :, g * V_DIM_A:(g + 1) * V_DIM_A] = og.astype(o_ref.dtype)


def flash_diff(q, rows, tiles, rel_bias, lam, subln, lam_init, blk):
    n, t, _ = q.shape
    nb = t // blk
    gw = GROUP_A * 2 * HEAD_DIM_A
    return pl.pallas_call(
        functools.partial(_flash_diff_kernel, blk=blk, lam_init=lam_init),
        grid=(n, N_KV_HEADS_A, nb, nb),
        in_specs=[
            pl.BlockSpec(memory_space=pltpu.SMEM),
            pl.BlockSpec((1, blk, gw), lambda b, c, i, j: (b, i, c)),
            pl.BlockSpec((1, blk, KV_ROW_A), lambda b, c, i, j: (b, jnp.minimum(j, i), c)),
            pl.BlockSpec((GROUP_A, 1, blk, blk), lambda b, c, i, j: (c, jnp.where(j < i, 1, 0), 0, 0)),
            pl.BlockSpec(lam.shape, lambda b, c, i, j: (0, 0)),
            pl.BlockSpec((1, V_DIM_A), lambda b, c, i, j: (0, 0)),
        ],
        out_specs=pl.BlockSpec((1, blk, GROUP_A * V_DIM_A), lambda b, c, i, j: (b, i, c)),
        out_shape=jax.ShapeDtypeStruct((n, t, N_HEADS_A * V_DIM_A), BF16),
        scratch_shapes=[pltpu.VMEM((2 * GROUP_A * blk, 2 * HEAD_DIM_A), BF16),
                        pltpu.VMEM((2 * GROUP_A * blk, 1), F32),
                        pltpu.VMEM((2 * GROUP_A * blk, 1), F32),
                        pltpu.VMEM((2 * GROUP_A * blk, V_DIM_A), F32)],
        compiler_params=_cparams(("parallel", "parallel", "parallel", "arbitrary")),
        name="flash_diff_prompt",
    )(rel_bias, q, rows, tiles, lam, subln.reshape(1, V_DIM_A))


def _decode_diff_kernel(pt_ref, wq_ref, *refs, pps, n_pages, lam_init):
    page_refs = refs[:pps]
    bias_ref, own_ref, lam_ref, sub_ref, o_ref, m_s, l_s, acc_s = refs[pps:]
    jc = pl.program_id(1)

    @pl.when(jc == 0)
    def _init():
        m_s[...] = jnp.full(m_s.shape, NEG_INF, F32)
        l_s[...] = jnp.zeros(l_s.shape, F32)
        acc_s[...] = jnp.zeros(acc_s.shape, F32)

    def page_part(ref, half):
        rows_per_slot = 2 * N_KV_HEADS_A
        return jnp.concatenate(
            [ref[0, 0, pl.ds(half * N_KV_HEADS_A + c, PAGE_SIZE, stride=rows_per_slot), :]
             for c in range(N_KV_HEADS_A)], axis=1)

    def own_part(row, half):
        w = 2 * HEAD_DIM_A
        return jnp.concatenate(
            [row[:, c * KV_ROW_A + half * w:c * KV_ROW_A + (half + 1) * w] for c in range(N_KV_HEADS_A)], axis=1)

    wq = wq_ref[0]
    s_list = [_dot_nt(wq, page_part(page_refs[k], 0)) + bias_ref[jc * pps + k] for k in range(pps)]
    m_cur = functools.reduce(jnp.maximum, [jnp.max(s, axis=1, keepdims=True) for s in s_list])
    m_prev = m_s[...]
    m_new = jnp.maximum(m_prev, m_cur)
    alpha = jnp.exp(m_prev - m_new)
    l_new = alpha * l_s[...]
    acc = alpha * acc_s[...]
    for k in range(pps):
        p = jnp.exp(s_list[k] - m_new)
        l_new = l_new + jnp.sum(p, axis=1, keepdims=True)
        acc = acc + _dot(p, page_part(page_refs[k], 1))
    m_s[...] = m_new
    l_s[...] = l_new
    acc_s[...] = acc

    @pl.when(jc == pl.num_programs(1) - 1)
    def _finish():
        own = own_ref[0]
        s_own = jnp.sum(wq * own_part(own, 0), axis=1, keepdims=True) + bias_ref[n_pages][:, 0:1]
        m_f = jnp.maximum(m_new, s_own)
        a = jnp.exp(m_new - m_f)
        p_own = jnp.exp(s_own - m_f)
        o = (a * acc + p_own * own_part(own, 1)) / (a * l_new + p_own)
        lam = _diff_lambda(lam_ref, lam_init)
        heads = []
        for c in range(N_KV_HEADS_A):
            for g in range(GROUP_A):
                r = c * 2 * GROUP_A + 2 * g
                o1 = o[r:r + 1, c * V_DIM_A:(c + 1) * V_DIM_A]
                o2 = o[r + 1:r + 2, c * V_DIM_A:(c + 1) * V_DIM_A]
                heads.append(_rms(o1 - lam * o2, sub_ref[...]) * (1.0 - lam_init))
        o_ref[0] = jnp.concatenate(heads, axis=1).astype(o_ref.dtype)


def decode_diff(wq, cache, layer, page_table, bias_rows, own_rows, lam, subln, lam_init, pps):
    nb, n_cols, f = wq.shape
    n_pages = page_table.shape[1]
    assert n_pages % pps == 0
    page_rows = cache.shape[2]
    row_w = own_rows.shape[1]

    def page_spec(k):
        return pl.BlockSpec((1, 1, page_rows, LANES),
                            lambda b, j, pt: (layer, pt[b * n_pages + j * pps + k], 0, 0))

    grid_spec = pltpu.PrefetchScalarGridSpec(
        num_scalar_prefetch=1,
        grid=(nb, n_pages // pps),
        in_specs=[pl.BlockSpec((1, n_cols, f), lambda b, j, pt: (b, 0, 0))]
        + [page_spec(k) for k in range(pps)]
        + [pl.BlockSpec(bias_rows.shape, lambda b, j, pt: (0, 0, 0)),
           pl.BlockSpec((1, 1, row_w), lambda b, j, pt: (b, 0, 0)),
           pl.BlockSpec(lam.shape, lambda b, j, pt: (0, 0)),
           pl.BlockSpec((1, V_DIM_A), lambda b, j, pt: (0, 0))],
        out_specs=pl.BlockSpec((1, 1, N_HEADS_A * V_DIM_A), lambda b, j, pt: (b, 0, 0)),
        scratch_shapes=[pltpu.VMEM((n_cols, 1), F32), pltpu.VMEM((n_cols, 1), F32),
                        pltpu.VMEM((n_cols, N_KV_HEADS_A * V_DIM_A), F32)],
    )
    return pl.pallas_call(
        functools.partial(_decode_diff_kernel, pps=pps, n_pages=n_pages, lam_init=lam_init),
        grid_spec=grid_spec,
        out_shape=jax.ShapeDtypeStruct((nb, 1, N_HEADS_A * V_DIM_A), BF16),
        compiler_params=_cparams(("parallel", "arbitrary")),
        name="decode_diff",
    )(page_table.reshape(-1), wq, *([cache] * pps), bias_rows, own_rows.reshape(nb, 1, row_w), lam,
      subln.reshape(1, V_DIM_A))


def _rope_table_kernel(inv_ref, cc_ref, ss_ref, *, pos0, stride):
    shape = cc_ref.shape
    pos = pos0 + stride * lax.broadcasted_iota(jnp.int32, shape, 0)
    lane = lax.broadcasted_iota(jnp.int32, shape, 1)
    ang = pos.astype(F32) * inv_ref[...]
    cc_ref[...] = jnp.cos(ang)
    sn = jnp.sin(ang)
    ss_ref[...] = jnp.where(lane % QK_ROPE_B < QK_ROPE_B // 2, -sn, sn)


def rope_tables(n_rows, pos0, stride):
    half = QK_ROPE_B // 2
    inv = ROPE_THETA ** (-jnp.arange(half, dtype=F32) * 2.0 / QK_ROPE_B)
    inv_lane = jnp.tile(inv, LANES // half).reshape(1, LANES)
    return pl.pallas_call(
        functools.partial(_rope_table_kernel, pos0=pos0, stride=stride),
        out_shape=[jax.ShapeDtypeStruct((n_rows, LANES), F32)] * 2,
        name="rope_tables",
    )(inv_lane)


def _mla_project_kernel(x_ref, cc_ref, ss_ref, ga_ref, wdq_ref, gq_ref, wn_ref, wr_ref, wrs_ref, wuk_ref,
                        wdkv_ref, gkv_ref, q_ref, rows_ref):
    cc = cc_ref[...]
    ss = ss_ref[...]
    h = _rms(x_ref[...], ga_ref[...]).astype(BF16)
    cq = _rms(_dot(h, wdq_ref[...]), gq_ref[...]).astype(BF16)
    for pair in range(N_HEADS_B // 2):
        qn = _dot(cq, wn_ref[:, pair * LANES:(pair + 1) * LANES]).astype(BF16)
        for hd in (2 * pair, 2 * pair + 1):
            q_ref[hd, :, 0:KV_LORA_B] = _dot(qn, wuk_ref[hd]).astype(q_ref.dtype)
    for hd in range(N_HEADS_B):
        a = _dot(cq, wr_ref[:, hd * LANES:(hd + 1) * LANES])
        b = _dot(cq, wrs_ref[:, hd * LANES:(hd + 1) * LANES])
        r = a * cc + b * ss
        q_ref[hd, :, KV_LORA_B:KV_ROW_B] = r[:, :QK_ROPE_B].astype(q_ref.dtype)
    kv = _dot(h, wdkv_ref[...])
    rows_ref[:, 0:KV_LORA_B] = _rms(kv[:, 0:KV_LORA_B], gkv_ref[...])
    r = kv[:, LANES:2 * LANES] * cc + kv[:, 2 * LANES:3 * LANES] * ss
    rows_ref[:, KV_LORA_B:KV_ROW_B] = r[:, :QK_ROPE_B]


def mla_project(x, cc, ss, g_attn, w):
    m, d = x.shape
    bm = _row_tile(m, 256)
    tb = cc.shape[0] // bm
    const = lambda *shape: pl.BlockSpec(shape, lambda i: (0,) * len(shape))
    return pl.pallas_call(
        _mla_project_kernel,
        grid=(m // bm,),
        in_specs=[pl.BlockSpec((bm, d), lambda i: (i, 0)),
                  pl.BlockSpec((bm, LANES), lambda i: (i % tb, 0)),
                  pl.BlockSpec((bm, LANES), lambda i: (i % tb, 0)),
                  const(1, d), const(*w["w_dq"].shape), const(1, w["w_dq"].shape[1]),
                  const(*w["w_n"].shape), const(*w["w_r"].shape), const(*w["w_rs"].shape),
                  const(*w["w_uk"].shape), const(*w["w_dkv"].shape), const(1, KV_LORA_B)],
        out_specs=[pl.BlockSpec((N_HEADS_B, bm, KV_ROW_B), lambda i: (0, i, 0)),
                   pl.BlockSpec((bm, KV_ROW_B), lambda i: (i, 0))],
        out_shape=[jax.ShapeDtypeStruct((N_HEADS_B, m, KV_ROW_B), BF16),
                   jax.ShapeDtypeStruct((m, KV_ROW_B), F32)],
        compiler_params=_cparams(("parallel",)),
        name="mla_project",
    )(x, cc, ss, g_attn.reshape(1, d), w["w_dq"], w["g_q"].reshape(1, -1), w["w_n"], w["w_r"], w["w_rs"],
      w["w_uk"], w["w_dkv"], w["g_kv"].reshape(1, -1))


def _flash_mla_kernel(q_ref, kv_ref, o_ref, m_s, l_s, acc_s, *, bq, bk):
    qi = pl.program_id(1)
    kj = pl.program_id(2)
    last = (qi * bq + bq - 1) // bk

    @pl.when(kj == 0)
    def _init():
        m_s[...] = jnp.full(m_s.shape, NEG_INF, F32)
        l_s[...] = jnp.zeros(l_s.shape, F32)
        acc_s[...] = jnp.zeros(acc_s.shape, F32)

    def step(masked):
        q = q_ref[...].reshape(N_HEADS_B * bq, KV_ROW_B)
        kv = kv_ref[...].astype(BF16)
        s = _dot_nt(q, kv) * MLA_SCALE
        if masked:
            qpos = qi * bq + lax.broadcasted_iota(jnp.int32, (bq, bk), 0)
            kpos = kj * bk + lax.broadcasted_iota(jnp.int32, (bq, bk), 1)
            s = jnp.where((kpos <= qpos)[None], s.reshape(N_HEADS_B, bq, bk), NEG_INF)
            s = s.reshape(N_HEADS_B * bq, bk)
        m_prev = m_s[...]
        m_new = jnp.maximum(m_prev, jnp.max(s, axis=1, keepdims=True))
        alpha = jnp.exp(m_prev - m_new)
        p = jnp.exp(s - m_new)
        l_s[...] = alpha * l_s[...] + jnp.sum(p, axis=1, keepdims=True)
        acc_s[...] = alpha * acc_s[...] + _dot(p.astype(BF16), kv[:, :KV_LORA_B])
        m_s[...] = m_new

    @pl.when(kj < last)
    def _full():
        step(False)

    @pl.when(kj == last)
    def _edge():
        step(True)
        o = acc_s[...] / l_s[...]
        for hd in range(N_HEADS_B):
            o_ref[:, hd * KV_LORA_B:(hd + 1) * KV_LORA_B] = o[hd * bq:(hd + 1) * bq].astype(o_ref.dtype)


def flash_mla(q_cat, rows, n, t, bq, bk):
    nq, nk = t // bq, t // bk
    return pl.pallas_call(
        functools.partial(_flash_mla_kernel, bq=bq, bk=bk),
        grid=(n, nq, nk),
        in_specs=[pl.BlockSpec((N_HEADS_B, bq, KV_ROW_B), lambda b, i, j: (0, b * nq + i, 0)),
                  pl.BlockSpec((bk, KV_ROW_B),
                               lambda b, i, j: (b * nk + jnp.minimum(j, (i * bq + bq - 1) // bk), 0))],
        out_specs=pl.BlockSpec((bq, N_HEADS_B * KV_LORA_B), lambda b, i, j: (b * nq + i, 0)),
        out_shape=jax.ShapeDtypeStruct((n * t, N_HEADS_B * KV_LORA_B), BF16),
        scratch_shapes=[pltpu.VMEM((N_HEADS_B * bq, 1), F32), pltpu.VMEM((N_HEADS_B * bq, 1), F32),
                        pltpu.VMEM((N_HEADS_B * bq, KV_LORA_B), F32)],
        compiler_params=_cparams(("parallel", "parallel", "arbitrary")),
        name="flash_mla_prompt",
    )(q_cat, rows)


def _decode_mla_kernel(pt_ref, wq_ref, *refs, pps):
    page_refs = refs[:pps]
    own_ref, o_ref, m_s, l_s, acc_s = refs[pps:]
    jc = pl.program_id(1)

    @pl.when(jc == 0)
    def _init():
        m_s[...] = jnp.full(m_s.shape, NEG_INF, F32)
        l_s[...] = jnp.zeros(l_s.shape, F32)
        acc_s[...] = jnp.zeros(acc_s.shape, F32)

    wq = wq_ref[0]
    s_list = [_dot(wq, page_refs[k][0, 0]) * MLA_SCALE for k in range(pps)]
    m_cur = functools.reduce(jnp.maximum, [jnp.max(s, axis=1, keepdims=True) for s in s_list])
    m_prev = m_s[...]
    m_new = jnp.maximum(m_prev, m_cur)
    alpha = jnp.exp(m_prev - m_new)
    l_new = alpha * l_s[...]
    acc = alpha * acc_s[...]
    for k in range(pps):
        p = jnp.exp(s_list[k] - m_new)
        l_new = l_new + jnp.sum(p, axis=1, keepdims=True)
        acc = acc + _dot_nt(p, page_refs[k][0, 0, 0:KV_LORA_B, :])
    m_s[...] = m_new
    l_s[...] = l_new
    acc_s[...] = acc

    @pl.when(jc == pl.num_programs(1) - 1)
    def _finish():
        own = own_ref[0]
        s_own = jnp.sum(wq * own, axis=1, keepdims=True) * MLA_SCALE
        m_f = jnp.maximum(m_new, s_own)
        a = jnp.exp(m_new - m_f)
        p_own = jnp.exp(s_own - m_f)
        o = (a * acc + p_own * own[:, 0:KV_LORA_B]) / (a * l_new + p_own)
        o_ref[0] = jnp.concatenate([o[hd:hd + 1] for hd in range(N_HEADS_B)], axis=1).astype(o_ref.dtype)


def decode_mla(wq, cache, layer, page_table, own_rows, pps):
    nb, n_heads, f = wq.shape
    n_pages = page_table.shape[1]
    assert n_pages % pps == 0

    def page_spec(k):
        return pl.BlockSpec((1, 1, f, PAGE_SIZE),
                            lambda b, j, pt: (layer, pt[b * n_pages + j * pps + k], 0, 0))

    grid_spec = pltpu.PrefetchScalarGridSpec(
        num_scalar_prefetch=1,
        grid=(nb, n_pages // pps),
        in_specs=[pl.BlockSpec((1, n_heads, f), lambda b, j, pt: (b, 0, 0))]
        + [page_spec(k) for k in range(pps)]
        + [pl.BlockSpec((1, 1, f), lambda b, j, pt: (b, 0, 0))],
        out_specs=pl.BlockSpec((1, 1, n_heads * KV_LORA_B), lambda b, j, pt: (b, 0, 0)),
        scratch_shapes=[pltpu.VMEM((n_heads, 1), F32), pltpu.VMEM((n_heads, 1), F32),
                        pltpu.VMEM((n_heads, KV_LORA_B), F32)],
    )
    return pl.pallas_call(
        functools.partial(_decode_mla_kernel, pps=pps),
        grid_spec=grid_spec,
        out_shape=jax.ShapeDtypeStruct((nb, 1, n_heads * KV_LORA_B), BF16),
        compiler_params=_cparams(("parallel", "arbitrary")),
        name="decode_mla",
    )(page_table.reshape(-1), wq, *([cache] * pps), own_rows.reshape(nb, 1, f))


def _mla_out_kernel(a_ref, wuv_ref, wo_ref, x_ref, o_ref):
    group = a_ref.shape[1] // wuv_ref.shape[0]
    o = jnp.concatenate(
        [_dot(a_ref[:, k * group:(k + 1) * group], wuv_ref[k]) for k in range(wuv_ref.shape[0])], axis=1)
    o_ref[...] = x_ref[...] + _dot(o.astype(BF16), wo_ref[...])


def mla_out(a, wuv, wo, x):
    m, k = a.shape
    d = wo.shape[1]
    bm = _row_tile(m, ROW_TILE)
    return pl.pallas_call(
        _mla_out_kernel,
        grid=(m // bm,),
        in_specs=[pl.BlockSpec((bm, k), lambda i: (i, 0)), pl.BlockSpec(wuv.shape, lambda i: (0, 0, 0)),
                  pl.BlockSpec(wo.shape, lambda i: (0, 0)), pl.BlockSpec((bm, d), lambda i: (i, 0))],
        out_specs=pl.BlockSpec((bm, d), lambda i: (i, 0)),
        out_shape=jax.ShapeDtypeStruct((m, d), F32),
        compiler_params=_cparams(("parallel",)),
        name="mla_out",
    )(a, wuv, wo, x)


def _silu(c):
    return c * (1.0 / (1.0 + jnp.exp(-c)))


def _ffn_prompt_kernel(x_ref, xh_ref, g_ref, wg_ref, wu_ref, wc_ref, bc_ref, wd_ref, y_ref, cs_ref,
                       hbuf, gbuf, acc, *, bm, tiles_per_seq):
    i = pl.program_id(0)
    j = pl.program_id(1)
    halo = BF16_SUBLANES

    @pl.when(j == 0)
    def _init():
        hbuf[halo:, :] = _rms(x_ref[...], g_ref[...]).astype(BF16)
        hh = _rms(xh_ref[...], g_ref[...]).astype(BF16)
        first = i % tiles_per_seq == 0
        hbuf[0:halo, :] = jnp.where(first, jnp.zeros_like(hh), hh)
        acc[...] = x_ref[...]

    gbuf[...] = _dot(hbuf[...], wg_ref[...])
    u = _dot(hbuf[halo:, :], wu_ref[...])
    c = bc_ref[...]
    for k in range(CONV_W):
        off = halo - (CONV_W - 1) + k
        c = c + wc_ref[k:k + 1, :] * gbuf[off:off + bm, :]
    act = (_silu(c) * u).astype(BF16)
    acc[...] += _dot(act, wd_ref[...])
    cs_ref[0] = gbuf[halo + bm - (CONV_W - 1):halo + bm, :]

    @pl.when(j == pl.num_programs(1) - 1)
    def _done():
        y_ref[...] = acc[...]


def ffn_prompt(x, t, g, wg, wu, wc, bc, wd):
    m, d = x.shape
    ff = wg.shape[1]
    bm = _row_tile(t, FFN_ROW_TILE)
    bn = FFN_COL_TILE
    assert ff % bn == 0 and bm % BF16_SUBLANES == 0
    tiles_per_seq = t // bm
    hb = bm // BF16_SUBLANES
    y, tails = pl.pallas_call(
        functools.partial(_ffn_prompt_kernel, bm=bm, tiles_per_seq=tiles_per_seq),
        grid=(m // bm, ff // bn),
        in_specs=[pl.BlockSpec((bm, d), lambda i, j: (i, 0)),
                  pl.BlockSpec((BF16_SUBLANES, d), lambda i, j: (jnp.maximum(i * hb - 1, 0), 0)),
                  pl.BlockSpec((1, d), lambda i, j: (0, 0)),
                  pl.BlockSpec((d, bn), lambda i, j: (0, j)),
                  pl.BlockSpec((d, bn), lambda i, j: (0, j)),
                  pl.BlockSpec((CONV_W, bn), lambda i, j: (0, j)),
                  pl.BlockSpec((1, bn), lambda i, j: (0, j)),
                  pl.BlockSpec((bn, d), lambda i, j: (j, 0))],
        out_specs=[pl.BlockSpec((bm, d), lambda i, j: (i, 0)),
                   pl.BlockSpec((1, CONV_W - 1, bn), lambda i, j: (i, 0, j))],
        out_shape=[jax.ShapeDtypeStruct((m, d), F32),
                   jax.ShapeDtypeStruct((m // bm, CONV_W - 1, ff), F32)],
        scratch_shapes=[pltpu.VMEM((bm + BF16_SUBLANES, d), BF16),
                        pltpu.VMEM((bm + BF16_SUBLANES, bn), F32),
                        pltpu.VMEM((bm, d), F32)],
        compiler_params=_cparams(("parallel", "arbitrary")),
        name="ffn_prompt",
    )(x, x, g.reshape(1, d), wg, wu, wc, bc.reshape(1, ff), wd)
    return y, tails[tiles_per_seq - 1::tiles_per_seq]


def _ffn_sample_kernel(x_ref, g_ref, p0_ref, p1_ref, wg_ref, wu_ref, wc_ref, bc_ref, wd_ref, y_ref, gate_ref,
                       hbuf, acc):
    j = pl.program_id(0)

    @pl.when(j == 0)
    def _init():
        hbuf[...] = _rms(x_ref[...], g_ref[...]).astype(BF16)
        acc[...] = x_ref[...]

    gate = _dot(hbuf[...], wg_ref[...])
    u = _dot(hbuf[...], wu_ref[...])
    c = bc_ref[...] + wc_ref[0:1, :] * p0_ref[...]
    c = c + wc_ref[1:2, :] * p1_ref[...]
    c = c + wc_ref[2:3, :] * gate
    acc[...] += _dot((_silu(c) * u).astype(BF16), wd_ref[...])
    gate_ref[...] = gate

    @pl.when(j == pl.num_programs(0) - 1)
    def _done():
        y_ref[...] = acc[...]


def ffn_sample(x, prev, g, wg, wu, wc, bc, wd):
    m, d = x.shape
    ff = wg.shape[1]
    bn = FFN_COL_TILE
    nj = ff // bn
    prev2 = prev.reshape(m, (CONV_W - 1) * ff)
    return pl.pallas_call(
        _ffn_sample_kernel,
        grid=(nj,),
        in_specs=[pl.BlockSpec((m, d), lambda j: (0, 0)),
                  pl.BlockSpec((1, d), lambda j: (0, 0)),
                  pl.BlockSpec((m, bn), lambda j: (0, j)),
                  pl.BlockSpec((m, bn), lambda j: (0, nj + j)),
                  pl.BlockSpec((d, bn), lambda j: (0, j)),
                  pl.BlockSpec((d, bn), lambda j: (0, j)),
                  pl.BlockSpec((CONV_W, bn), lambda j: (0, j)),
                  pl.BlockSpec((1, bn), lambda j: (0, j)),
                  pl.BlockSpec((bn, d), lambda j: (j, 0))],
        out_specs=[pl.BlockSpec((m, d), lambda j: (0, 0)), pl.BlockSpec((m, bn), lambda j: (0, j))],
        out_shape=[jax.ShapeDtypeStruct((m, d), F32), jax.ShapeDtypeStruct((m, ff), F32)],
        scratch_shapes=[pltpu.VMEM((m, d), BF16), pltpu.VMEM((m, d), F32)],
        compiler_params=_cparams(("arbitrary",)),
        name="ffn_sample",
    )(x, g.reshape(1, d), prev2, prev2, wg, wu, wc, bc.reshape(1, ff), wd)


def _mla_weights(w_dq, g_q, w_uq, w_dkv, g_kv, w_uk, w_uv, w_o):
    lora = w_uq.shape[0]
    half = QK_ROPE_B // 2
    uq = w_uq.reshape(lora, N_HEADS_B, QK_NOPE_B + QK_ROPE_B)
    rope = uq[:, :, QK_NOPE_B:]
    pad = ((0, 0), (0, 0), (0, LANES - QK_ROPE_B))
    swapped = jnp.concatenate([rope[:, :, half:], rope[:, :, :half]], -1)
    uk = jnp.transpose(w_uk, (1, 2, 0))
    z = jnp.zeros_like(uk)
    odd = (jnp.arange(N_HEADS_B) % 2 == 1)[:, None, None]
    uk_pad = jnp.concatenate([jnp.where(odd, z, uk), jnp.where(odd, uk, z)], axis=1)
    d = w_dkv.shape[0]
    zc = lambda n: jnp.zeros((d, n), w_dkv.dtype)
    dkv = jnp.concatenate([w_dkv[:, :KV_LORA_B], w_dkv[:, KV_LORA_B:], zc(LANES - QK_ROPE_B),
                           w_dkv[:, KV_LORA_B + half:], w_dkv[:, KV_LORA_B:KV_LORA_B + half],
                           zc(LANES - QK_ROPE_B)], axis=1)
    per = 4
    uv = jnp.transpose(w_uv, (1, 0, 2)).reshape(N_HEADS_B // per, per, KV_LORA_B, V_DIM_B)
    eye = jnp.eye(per, dtype=w_uv.dtype)
    uv_bd = jnp.einsum("kacd,ab->kacbd", uv, eye).reshape(N_HEADS_B // per, per * KV_LORA_B, per * V_DIM_B)
    return dict(
        w_dq=w_dq.astype(BF16), g_q=g_q,
        w_n=uq[:, :, :QK_NOPE_B].reshape(lora, -1).astype(BF16),
        w_r=jnp.pad(rope, pad).reshape(lora, -1).astype(BF16),
        w_rs=jnp.pad(swapped, pad).reshape(lora, -1).astype(BF16),
        w_uk=uk_pad.astype(BF16), w_dkv=dkv.astype(BF16), g_kv=g_kv,
        w_uv=uv_bd.astype(BF16), w_o=w_o.astype(BF16))


def _diff_query_matrix(q):
    nb = q.shape[0]
    q5 = q.reshape(nb, N_KV_HEADS_A, GROUP_A, 2, HEAD_DIM_A)
    eye_c = jnp.eye(N_KV_HEADS_A, dtype=q.dtype)
    eye_m = jnp.eye(2, dtype=q.dtype)
    w = jnp.einsum("ncgmd,cC,mM->ncgmCMd", q5, eye_c, eye_m)
    return w.reshape(nb, 2 * N_HEADS_A, N_KV_HEADS_A * 2 * HEAD_DIM_A)


def kernel(x_prompt, x_sample, cache_diff_kv, cache_mla, state_ffn_conv, page_table,
           rel_bias, g_attn, g_ffn, g_final,
           w_q_a, w_kv_a, lam_q1, lam_k1, lam_q2, lam_k2, subln_a, w_o_a,
           w_dq_b, g_q_b, w_uq_b, w_dkv_b, g_kv_b, w_uk_b, w_uv_b, w_o_b,
           w_gate, w_up, w_conv, b_conv, w_down):
    n, t, d = x_prompt.shape
    nb = x_sample.shape[0]
    assert x_sample.shape[1] == 1
    depth = g_attn.shape[0]
    n_pages = page_table.shape[1]
    past = n_pages * PAGE_SIZE
    blk = min(FLASH_B, t)
    assert blk >= MAX_DISTANCE and t % blk == 0

    xp = x_prompt.reshape(n * t, d)
    xs = x_sample.reshape(nb, d)
    la, npg = cache_diff_kv.shape[:2]
    cache_a = cache_diff_kv.reshape(la, npg, PAGE_SIZE, N_KV_HEADS_A, 2, 2 * HEAD_DIM_A)
    cache_a = cache_a.transpose(0, 1, 2, 4, 3, 5).reshape(la, npg, PAGE_SIZE * 2 * N_KV_HEADS_A, 2 * HEAD_DIM_A)
    cache_b = jnp.swapaxes(cache_mla, 2, 3)

    tiles, dec_bias = bias_tables(rel_bias, blk, n_pages)
    dec_bias = jnp.repeat(jnp.transpose(dec_bias, (1, 0, 2)), 2, axis=1)
    cc_p, ss_p = rope_tables(t, 0, 1)
    cc_s, ss_s = rope_tables(nb, past, 0)

    diff_p, diff_s, mla_p, mla_s, conv_p, conv_s = [], [], [], [], [], []
    for i in range(depth):
        j = i // 2
        if i % 2 == 0:
            lam_init = 0.8 - 0.6 * math.exp(-0.3 * i)
            lam = jnp.stack([lam_q1[j], lam_k1[j], lam_q2[j], lam_k2[j]])
            ws = [w_q_a[j].astype(BF16), w_kv_a[j].astype(BF16)]
            scale = HEAD_DIM_A ** -0.5
            w_o = w_o_a[j].astype(BF16)
            q, rows = norm_linear(xp, g_attn[i], ws, (scale, 1.0), (BF16, F32))
            o = flash_diff(q.reshape(n, t, -1), rows.reshape(n, t, -1), tiles, rel_bias, lam, subln_a[j],
                           lam_init, blk)
            xp = linear_residual(o.reshape(n * t, -1), w_o, xp)
            diff_p.append(rows.reshape(n, t, N_KV_HEADS_A, KV_ROW_A))
            q, rows = norm_linear(xs, g_attn[i], ws, (scale, 1.0), (F32, F32))
            o = decode_diff(_diff_query_matrix(q), cache_a, j, page_table, dec_bias, rows, lam, subln_a[j],
                            lam_init, min(DIFF_PAGES_PER_STEP, n_pages))
            xs = linear_residual(o.reshape(nb, -1), w_o, xs)
            diff_s.append(rows.reshape(nb, 1, N_KV_HEADS_A, KV_ROW_A))
        else:
            w = _mla_weights(w_dq_b[j], g_q_b[j], w_uq_b[j], w_dkv_b[j], g_kv_b[j], w_uk_b[j], w_uv_b[j],
                             w_o_b[j])
            q_cat, rows = mla_project(xp, cc_p, ss_p, g_attn[i], w)
            o = flash_mla(q_cat, rows, n, t, min(MLA_BQ, t), min(MLA_BK, t))
            xp = mla_out(o, w["w_uv"], w["w_o"], xp)
            mla_p.append(rows.reshape(n, t, KV_ROW_B))
            q_cat, rows = mla_project(xs, cc_s, ss_s, g_attn[i], w)
            wq = jnp.transpose(q_cat, (1, 0, 2)).astype(F32)
            o = decode_mla(wq, cache_b, j, page_table, rows, min(MLA_PAGES_PER_STEP, n_pages))
            xs = mla_out(o.reshape(nb, -1), w["w_uv"], w["w_o"], xs)
            mla_s.append(rows.reshape(nb, 1, KV_ROW_B))
        wg, wu, wd = w_gate[i].astype(BF16), w_up[i].astype(BF16), w_down[i].astype(BF16)
        xp, cp = ffn_prompt(xp, t, g_ffn[i], wg, wu, w_conv[i], b_conv[i], wd)
        xs, gate = ffn_sample(xs, state_ffn_conv[i], g_ffn[i], wg, wu, w_conv[i], b_conv[i], wd)
        conv_p.append(cp)
        conv_s.append(jnp.concatenate([state_ffn_conv[i][:, 1:], gate[:, None, :]], axis=1))
    y_prompt = rmsnorm(xp, g_final).reshape(n, t, d)
    y_sample = rmsnorm(xs, g_final).reshape(nb, 1, d)
    return (y_prompt, y_sample, jnp.stack(diff_p), jnp.stack(diff_s), jnp.stack(mla_p), jnp.stack(mla_s),
            jnp.stack(conv_p), jnp.stack(conv_s))
```
